```python
import jax, jax.numpy as jnp
from jax import lax
import numpy as np

D_MODEL = 1024
BATCH = 2
SEQ = 16384
DEPTH = 1

MIX_WIDTH = D_MODEL
GLA_HEADS = 4
GLA_DV = (MIX_WIDTH // 2) // GLA_HEADS
GLA_DK = GLA_DV // 2
GLA_GATE_RANK = 16
GLA_GATE_NORM = 16.0
GLA_CHUNK = 16
RET_HEADS = 4
RET_DK = (MIX_WIDTH // 2) // RET_HEADS
RET_DV = RET_DK
RET_CHUNK = 128
ROPE_BASE = 10000.0
N_EXPERTS = 32
TOP_K = 4
D_EXPERT = D_MODEL
SWIGLU_LIMIT = 7.0
SWIGLU_ALPHA = 1.702
EPS = 1e-6

GLA_QK = GLA_HEADS * GLA_DK
GLA_VW = GLA_HEADS * GLA_DV
RET_QK = RET_HEADS * RET_DK
RET_VW = RET_HEADS * RET_DV
MIX_IN = 2 * GLA_QK + 2 * GLA_VW + GLA_GATE_RANK + 2 * RET_QK + 2 * RET_VW

kernel_name = 'hymba_gla_retnet_moe_adaln'


def rmsnorm(x, w):
    xf = x.astype(jnp.float32)
    y = xf * lax.rsqrt(jnp.mean(xf * xf, axis=-1, keepdims=True) + EPS)
    return (y * w.astype(jnp.float32)).astype(x.dtype)


def layernorm_nobias(x, w):
    xf = x.astype(jnp.float32)
    mu = jnp.mean(xf, axis=-1, keepdims=True)
    var = jnp.mean(jnp.square(xf - mu), axis=-1, keepdims=True)
    return ((xf - mu) * lax.rsqrt(var + EPS) * w.astype(jnp.float32))


def rotary(x, positions):
    half = x.shape[-1] // 2
    inv = ROPE_BASE ** (-jnp.arange(half, dtype=jnp.float32) / half)
    ang = positions.astype(jnp.float32)[..., None] * inv
    cos = jnp.cos(ang)[:, :, None, :]
    sin = jnp.sin(ang)[:, :, None, :]
    x1, x2 = x[..., :half].astype(jnp.float32), x[..., half:].astype(jnp.float32)
    return jnp.concatenate([x1 * cos - x2 * sin, x1 * sin + x2 * cos], axis=-1).astype(x.dtype)


def chunk_state_scan(decay, U):
    def step(S, inp):
        d, u = inp
        return d * S + u, S
    _, prev = lax.scan(step, jnp.zeros_like(U[0]), (decay, U))
    return prev


def gla_chunked(q, k, v, log_a):
    B, S, H, dk = q.shape
    dv = v.shape[-1]
    C = GLA_CHUNK
    N = S // C
    q = q.reshape(B, N, C, H, dk)
    k = k.reshape(B, N, C, H, dk)
    v = v.reshape(B, N, C, H, dv)
    b = jnp.cumsum(log_a.astype(jnp.float32).reshape(B, N, C, H, dk), axis=2)
    causal = jnp.tril(jnp.ones((C, C), dtype=bool))[:, :, None, None]
    diff = b[:, :, :, None] - b[:, :, None, :]
    decay = jnp.exp(jnp.where(causal, diff, -jnp.inf))
    A = jnp.einsum('bntshd,bnshd->bnhts', decay * q[:, :, :, None], k)
    o_intra = jnp.einsum('bnhts,bnshv->bnthv', A, v)
    b_last = b[:, :, -1]
    k_up = k * jnp.exp(b_last[:, :, None] - b)
    U = jnp.einsum('bnshd,bnshv->nbhdv', k_up, v).astype(jnp.float32)
    chunk_decay = jnp.exp(b_last).transpose(1, 0, 2, 3)[..., None]
    S_prev = chunk_state_scan(chunk_decay, U)
    o_inter = jnp.einsum('bnthd,nbhdv->bnthv', q * jnp.exp(b), S_prev)
    return (o_intra + o_inter).reshape(B, S, H, dv)


def retention_chunked(q, k, v, log_gamma):
    B, S, H, dk = q.shape
    dv = v.shape[-1]
    C = RET_CHUNK
    N = S // C
    q = q.reshape(B, N, C, H, dk)
    k = k.reshape(B, N, C, H, dk)
    v = v.reshape(B, N, C, H, dv)
    idx = jnp.arange(C, dtype=jnp.float32)
    rel = idx[:, None] - idx[None, :]
    Dmask = jnp.where((rel >= 0)[None], jnp.exp(rel[None] * log_gamma[:, None, None]), 0.0)
    scores = jnp.einsum('bnthd,bnshd->bnhts', q, k) * Dmask
    o_intra = jnp.einsum('bnhts,bnshv->bnthv', scores, v)
    xi = jnp.exp((idx + 1.0)[:, None] * log_gamma[None, :])
    zeta = jnp.exp((C - 1.0 - idx)[:, None] * log_gamma[None, :])
    U = jnp.einsum('bnshd,bnshv->nbhdv', k * zeta[None, None, :, :, None], v).astype(jnp.float32)
    chunk_decay = jnp.broadcast_to(jnp.exp(C * log_gamma)[None, None, :, None, None], (N, 1, H, 1, 1))
    R_prev = chunk_state_scan(chunk_decay, U)
    o_inter = jnp.einsum('bnthd,nbhdv->bnthv', q * xi[None, None, :, :, None], R_prev)
    return (o_intra + o_inter).reshape(B, S, H, dv)


def hybrid_mixer(h, positions, w_in, w_gk2, b_gk, gla_onorm, ret_onorm, w_out):
    B, S, _ = h.shape
    sizes = [GLA_QK, GLA_QK, GLA_VW, GLA_VW, GLA_GATE_RANK, RET_QK, RET_QK, RET_VW, RET_VW]
    cuts = np.cumsum(sizes)[:-1].tolist()
    proj = h @ w_in
    gq, gk, gv, gg, glr, rq, rk, rv, rg = jnp.split(proj, cuts, axis=-1)
    gq = gq.reshape(B, S, GLA_HEADS, GLA_DK) * (GLA_DK ** -0.5)
    gk = gk.reshape(B, S, GLA_HEADS, GLA_DK)
    gv = gv.reshape(B, S, GLA_HEADS, GLA_DV)
    log_a = jax.nn.log_sigmoid((glr @ w_gk2 + b_gk).astype(jnp.float32)) / GLA_GATE_NORM
    log_a = log_a.reshape(B, S, GLA_HEADS, GLA_DK)
    o_gla = gla_chunked(gq, gk, gv, log_a)
    o_gla = rmsnorm(o_gla.astype(jnp.float32), gla_onorm) * jax.nn.silu(gg.reshape(B, S, GLA_HEADS, GLA_DV).astype(jnp.float32))
    log_gamma = jnp.log(1.0 - jnp.exp2(-5.0 - jnp.arange(RET_HEADS, dtype=jnp.float32)))
    rq = rotary(rq.reshape(B, S, RET_HEADS, RET_DK), positions)
    rk = rotary(rk.reshape(B, S, RET_HEADS, RET_DK), positions) * (RET_DK ** -0.5)
    rv = rv.reshape(B, S, RET_HEADS, RET_DV)
    o_ret = retention_chunked(rq, rk, rv, log_gamma)
    o_ret = layernorm_nobias(o_ret, ret_onorm) * jax.nn.silu(rg.reshape(B, S, RET_HEADS, RET_DV).astype(jnp.float32))
    o = jnp.concatenate([o_gla.reshape(B, S, GLA_VW), o_ret.reshape(B, S, RET_VW)], axis=-1).astype(h.dtype)
    return o @ w_out


def moe_ffn(h, w_router, b_router, w_gu, b_gu, w_down, b_down):
    B, S, D = h.shape
    T = B * S
    hf = h.reshape(T, D)
    logits = (hf @ w_router + b_router).astype(jnp.float32)
    top_val, top_idx = lax.top_k(logits, TOP_K)
    weights = jax.nn.softmax(top_val, axis=-1).astype(h.dtype)
    flat_e = top_idx.reshape(-1)
    order = jnp.argsort(flat_e)
    sorted_e = flat_e[order]
    tok = order // TOP_K
    group_sizes = jnp.bincount(flat_e, length=N_EXPERTS).astype(jnp.int32)
    xs = hf[tok]
    gu = lax.ragged_dot(xs, w_gu, group_sizes) + b_gu[sorted_e]
    gate, up = jnp.split(gu, 2, axis=-1)
    gate = jnp.minimum(gate, SWIGLU_LIMIT)
    up = jnp.clip(up, -SWIGLU_LIMIT, SWIGLU_LIMIT)
    act = (up + 1.0) * (gate * jax.nn.sigmoid(SWIGLU_ALPHA * gate))
    out = lax.ragged_dot(act.astype(h.dtype), w_down, group_sizes) + b_down[sorted_e]
    out = out * weights.reshape(-1)[order][:, None]
    y = jnp.zeros((T, D), dtype=out.dtype).at[tok].add(out)
    return y.reshape(B, S, D).astype(h.dtype)


def setup_inputs(seed: int = 0) -> dict:
    key = jax.random.key(seed)
    ks = jax.random.split(key, 24)
    f32 = jnp.float32
    L, D, E, F = DEPTH, D_MODEL, N_EXPERTS, D_EXPERT
    nrm = lambda k, shape, fan: jax.random.normal(k, shape, f32) * (fan ** -0.5)
    x = jax.random.normal(ks[0], (BATCH, SEQ, D), f32)
    c = jax.random.normal(ks[1], (BATCH, D), f32)
    positions = (jnp.arange(SEQ, dtype=jnp.int32)[None, :]
                 + jax.random.randint(ks[2], (BATCH, 1), 0, 1024, dtype=jnp.int32))
    return {
        'x': x,
        'c': c,
        'positions': positions,
        'w_ada': nrm(ks[3], (L, D, 6 * D), D),
        'b_ada': 0.02 * jax.random.normal(ks[4], (L, 6 * D), f32),
        'norm1': 1.0 + 0.05 * jax.random.normal(ks[5], (L, D), f32),
        'w_in': nrm(ks[6], (L, D, MIX_IN), D),
        'w_gk2': nrm(ks[7], (L, GLA_GATE_RANK, GLA_QK), GLA_GATE_RANK),
        'b_gk': 0.1 * jax.random.normal(ks[8], (L, GLA_QK), f32),
        'gla_onorm': 1.0 + 0.05 * jax.random.normal(ks[9], (L, GLA_DV), f32),
        'ret_onorm': 1.0 + 0.05 * jax.random.normal(ks[10], (L, RET_DV), f32),
        'w_out': nrm(ks[11], (L, MIX_WIDTH, D), MIX_WIDTH),
        'norm2': 1.0 + 0.05 * jax.random.normal(ks[12], (L, D), f32),
        'w_router': nrm(ks[13], (L, D, E), D),
        'b_router': 0.01 * jax.random.normal(ks[14], (L, E), f32),
        'w_gu': nrm(ks[15], (L, E, D, 2 * F), D),
        'b_gu': 0.02 * jax.random.normal(ks[16], (L, E, 2 * F), f32),
        'w_down': nrm(ks[17], (L, E, F, D), F),
        'b_down': 0.02 * jax.random.normal(ks[18], (L, E, D), f32),
        'norm_f': 1.0 + 0.05 * jax.random.normal(ks[19], (D,), f32),
    }


def reference(x, c, positions, w_ada, b_ada, norm1, w_in, w_gk2, b_gk, gla_onorm, ret_onorm,
              w_out, norm2, w_router, b_router, w_gu, b_gu, w_down, b_down, norm_f):
    for l in range(DEPTH):
        mod = jax.nn.silu(c) @ w_ada[l] + b_ada[l]
        sh1, sc1, g1, sh2, sc2, g2 = jnp.split(mod[:, None, :], 6, axis=-1)
        h = rmsnorm(x, norm1[l]) * (1.0 + sc1) + sh1
        x = x + g1 * hybrid_mixer(h, positions, w_in[l], w_gk2[l], b_gk[l],
                                  gla_onorm[l], ret_onorm[l], w_out[l])
        h = rmsnorm(x, norm2[l]) * (1.0 + sc2) + sh2
        x = x + g2 * moe_ffn(h, w_router[l], b_router[l], w_gu[l], b_gu[l], w_down[l], b_down[l])
    return rmsnorm(x, norm_f)
```

```python
import functools
import math

import jax
import jax.numpy as jnp
import numpy as np
from jax import lax
from jax.experimental import pallas as pl
from jax.experimental.pallas import tpu as pltpu

F32 = jnp.float32
BF16 = jnp.bfloat16
I32 = jnp.int32

D_MODEL = 1024
GLA_HEADS = 4
GLA_DK = 64
GLA_DV = 128
GLA_GATE_RANK = 16
GLA_GATE_NORM = 16.0
RET_HEADS = 4
RET_DK = 128
RET_DV = 128
ROPE_BASE = 10000.0
N_EXPERTS = 32
TOP_K = 4
SWIGLU_LIMIT = 7.0
SWIGLU_ALPHA = 1.702
EPS = 1e-6
GLA_QK = GLA_HEADS * GLA_DK
GLA_VW = GLA_HEADS * GLA_DV
RET_QK = RET_HEADS * RET_DK
RET_VW = RET_HEADS * RET_DV

LANES = 128
V7X_VMEM_BYTES = 64 * 1024 * 1024

CHUNK = 128
TM_PROJ = 512
TM_ROW = 256


def _vmem_limit(nbytes):
    return int(min(nbytes, V7X_VMEM_BYTES - 8 * 1024 * 1024))


def _dot(a, b):
    return jnp.dot(a, b, preferred_element_type=F32)


def _dot_nt(a, b):
    return lax.dot_general(a, b, (((1,), (1,)), ((), ())), preferred_element_type=F32)


def _dot_tn(a, b):
    return lax.dot_general(a, b, (((0,), (0,)), ((), ())), preferred_element_type=F32)


def _silu(x):
    return x * jax.nn.sigmoid(x)


def _pack_bf16_pair(lo, hi):
    lo_b = lax.bitcast_convert_type(lo.astype(BF16).astype(F32), jnp.uint32)
    hi_b = lax.bitcast_convert_type(hi.astype(BF16).astype(F32), jnp.uint32)
    word = (hi_b & jnp.uint32(0xFFFF0000)) | (lo_b >> 16)
    return lax.bitcast_convert_type(word, I32)


def _unpack_bf16_pair(word_i32):
    word = lax.bitcast_convert_type(word_i32, jnp.uint32)
    lo = lax.bitcast_convert_type(word << 16, F32)
    hi = lax.bitcast_convert_type(word & jnp.uint32(0xFFFF0000), F32)
    return lo, hi


def _ada_kernel(c_ref, w_ref, b_ref, o_ref):
    s = _silu(c_ref[...]).astype(BF16)
    o_ref[...] = _dot(s, w_ref[...].astype(BF16)) + b_ref[...]


def _ada(c_pad, w_ada, b_ada):
    rows, d = c_pad.shape
    n = w_ada.shape[1]
    tn = 1024
    return pl.pallas_call(
        _ada_kernel,
        grid=(n // tn,),
        in_specs=[
            pl.BlockSpec((rows, d), lambda j: (0, 0)),
            pl.BlockSpec((d, tn), lambda j: (0, j)),
            pl.BlockSpec((1, tn), lambda j: (0, j)),
        ],
        out_specs=pl.BlockSpec((rows, tn), lambda j: (0, j)),
        out_shape=jax.ShapeDtypeStruct((rows, n), F32),
        compiler_params=pltpu.CompilerParams(dimension_semantics=("arbitrary",)),
        name="ada",
    )(c_pad, w_ada, b_ada)


_O_GQ, _O_GK, _O_GV, _O_GG = 0, 256, 512, 1024
_O_RQ, _O_RK, _O_RV, _O_RG, _O_LR = 1536, 2048, 2560, 3072, 3584
_W_IN_COLS = 3584 + LANES


def _inproj_kernel(x_ref, sc_ref, sh_ref, n1_ref, pos_ref, w_ref, wgk_ref, bgk_ref,
                   gq_ref, gk_ref, gv_ref, gg_ref, la_ref, rq_ref, rk_ref, rv_ref, rg_ref):
    x = x_ref[...]
    ms = jnp.mean(x * x, axis=-1, keepdims=True)
    h = (x * lax.rsqrt(ms + EPS)) * n1_ref[...]
    h = h * (1.0 + sc_ref[0]) + sh_ref[0]
    hb = h.astype(BF16)

    def proj(off, width):
        return _dot(hb, w_ref[:, off:off + width])

    gq_ref[...] = (proj(_O_GQ, GLA_QK) * (GLA_DK ** -0.5)).astype(BF16)
    gk_ref[...] = proj(_O_GK, GLA_QK).astype(BF16)
    gv_ref[...] = proj(_O_GV, GLA_VW).astype(BF16)
    gg_ref[...] = _silu(proj(_O_GG, GLA_VW)).astype(BF16)
    rv_ref[...] = proj(_O_RV, RET_VW).astype(BF16)
    rg_ref[...] = _silu(proj(_O_RG, RET_VW)).astype(BF16)

    glr = proj(_O_LR, LANES).astype(BF16)
    z = _dot(glr, wgk_ref[...]) + bgk_ref[...]
    la_ref[...] = (jnp.minimum(z, 0.0) - jnp.log1p(jnp.exp(-jnp.abs(z)))) * (1.0 / GLA_GATE_NORM)

    half = RET_DK // 2
    lane = lax.broadcasted_iota(I32, (1, RET_DK), 1)
    freq = (lane % half).astype(F32) * (1.0 / half)
    inv = jnp.exp(freq * (-math.log(ROPE_BASE)))
    ang = pos_ref[...].astype(F32) * inv
    cos = jnp.cos(ang)
    sin = jnp.sin(ang)
    sin_signed = jnp.where(lane < half, -sin, sin)
    rq = proj(_O_RQ, RET_QK)
    rk = proj(_O_RK, RET_QK)
    for hd in range(RET_HEADS):
        sl = slice(hd * RET_DK, (hd + 1) * RET_DK)
        q_h = rq[:, sl]
        k_h = rk[:, sl]
        rq_ref[:, sl] = (q_h * cos + pltpu.roll(q_h, half, 1) * sin_signed).astype(BF16)
        rk_ref[:, sl] = ((k_h * cos + pltpu.roll(k_h, half, 1) * sin_signed)
                         * (RET_DK ** -0.5)).astype(BF16)


def _inproj(x2, sc1, sh1, n1, pos_col, w_cat, wgk, bgk, tiles_per_batch):
    t, d = x2.shape
    tm = TM_PROJ
    tok = lambda w: pl.BlockSpec((tm, w), lambda i: (i, 0))
    per_batch = pl.BlockSpec((1, 1, d), lambda i: (i // tiles_per_batch, 0, 0))
    full = lambda a: pl.BlockSpec(a.shape, lambda i: (0,) * a.ndim)
    out_widths = [GLA_QK, GLA_QK, GLA_VW, GLA_VW, GLA_QK, RET_QK, RET_QK, RET_VW, RET_VW]
    out_dtypes = [BF16, BF16, BF16, BF16, F32, BF16, BF16, BF16, BF16]
    est = 2 * (tm * d * 4 + w_cat.size * 2 + sum(tm * w * 4 for w in out_widths)) + 16 * tm * 512 * 4
    return pl.pallas_call(
        _inproj_kernel,
        grid=(t // tm,),
        in_specs=[tok(d), per_batch, per_batch, full(n1), tok(1), full(w_cat), full(wgk), full(bgk)],
        out_specs=[tok(w) for w in out_widths],
        out_shape=[jax.ShapeDtypeStruct((t, w), dt) for w, dt in zip(out_widths, out_dtypes)],
        compiler_params=pltpu.CompilerParams(dimension_semantics=("arbitrary",),
                                             vmem_limit_bytes=_vmem_limit(est)),
        name="inproj",
    )(x2, sc1, sh1, n1, pos_col, w_cat, wgk, bgk)


_GLA_LEVELS = int(math.log2(CHUNK))


def _gla_structure():
    c = CHUNK
    t = np.arange(c)[:, None]
    u = np.arange(c)[None, :]
    groups = [u <= t, u > t]
    q_side, k_side, masks = [], [], []
    half = c // 2
    while half >= 1:
        blk = 2 * half
        mid = t - (t % blk) + half
        upper = (t % blk) >= half
        q_side.append(upper & (u >= mid) & (u <= t))
        k_side.append((~upper) & (u > t) & (u <= mid - 1))
        masks.append((t // blk == u // blk) & upper & ((u % blk) < half))
        half //= 2
    masks.append(np.eye(c, dtype=bool))
    sums = np.concatenate(groups + q_side + k_side, axis=0).astype(np.float32)
    return sums, np.stack(masks).astype(np.float32)


def _mixer_kernel(gq_ref, gk_ref, gv_ref, gg_ref, la_ref, rq_ref, rk_ref, rv_ref, rg_ref,
                  sums_ref, mask_ref, gon_ref, ron_ref, o_ref,
                  gstate_ref, rstate_ref, dmask_ref, xi_ref, zeta_ref):
    c = CHUNK
    chunk_id = pl.program_id(1)
    log_gamma = [math.log1p(-(2.0 ** (-5 - h))) for h in range(RET_HEADS)]

    @pl.when((pl.program_id(0) == 0) & (chunk_id == 0))
    def _init_decay_tables():
        t = lax.broadcasted_iota(I32, (c, c), 0)
        s = lax.broadcasted_iota(I32, (c, c), 1)
        rel = (t - s).astype(F32)
        tf = t.astype(F32)
        for h in range(RET_HEADS):
            dmask_ref[h] = jnp.where(rel >= 0.0, jnp.exp(jnp.maximum(rel, 0.0) * log_gamma[h]), 0.0)
            xi_ref[h] = jnp.exp((tf + 1.0) * log_gamma[h])
            zeta_ref[h] = jnp.exp((c - 1.0 - tf) * log_gamma[h])

    @pl.when(chunk_id == 0)
    def _reset_state():
        gstate_ref[...] = jnp.zeros_like(gstate_ref)
        rstate_ref[...] = jnp.zeros_like(rstate_ref)

    la = la_ref[...]
    la_hi = la.astype(BF16)
    la_lo = (la - la_hi.astype(F32)).astype(BF16)
    expo = _dot(sums_ref[...], jnp.concatenate([la_hi, la_lo], axis=0))
    qb = gq_ref[...]
    kb = gk_ref[...]
    q = qb.astype(F32)
    k = kb.astype(F32)
    b = expo[0:c]
    q_inter = (q * jnp.exp(b)).astype(BF16)
    k_state = (k * jnp.exp(expo[c:2 * c])).astype(BF16)
    decay_last = jnp.exp(b[c - 1:c, :])
    q_lv, k_lv = [], []
    for l in range(_GLA_LEVELS):
        q_lv.append((q * jnp.exp(expo[(2 + l) * c:(3 + l) * c])).astype(BF16))
        k_lv.append((k * jnp.exp(expo[(2 + _GLA_LEVELS + l) * c:(3 + _GLA_LEVELS + l) * c])).astype(BF16))
    q_lv.append(qb)
    k_lv.append(kb)

    lane = lax.broadcasted_iota(I32, (c, LANES), 1)
    head_lanes = [lane < GLA_DK, lane >= GLA_DK]
    zero_b = jnp.zeros((c, LANES), BF16)
    for g in range(GLA_HEADS // 2):
        gl = slice(g * LANES, (g + 1) * LANES)
        state_t = gstate_ref[g]
        state_b = state_t.astype(BF16)
        new_state = state_t * decay_last[:, gl]
        for hh in range(2):
            h = 2 * g + hh
            vl = slice(h * GLA_DV, (h + 1) * GLA_DV)
            keep = head_lanes[hh]
            attn = jnp.zeros((c, c), F32)
            for l in range(_GLA_LEVELS + 1):
                q_h = jnp.where(keep, q_lv[l][:, gl], zero_b)
                attn = attn + mask_ref[l] * _dot_nt(q_h, k_lv[l][:, gl])
            v_h = gv_ref[:, vl]
            o_h = _dot(attn.astype(BF16), v_h)
            o_h = o_h + _dot_nt(jnp.where(keep, q_inter[:, gl], zero_b), state_b)
            new_state = new_state + _dot_tn(v_h, jnp.where(keep, k_state[:, gl], zero_b))
            ms = jnp.mean(o_h * o_h, axis=-1, keepdims=True)
            o_h = o_h * lax.rsqrt(ms + EPS) * gon_ref[...] * gg_ref[:, vl].astype(F32)
            o_ref[:, vl] = o_h.astype(BF16)
        gstate_ref[g] = new_state

    for h in range(RET_HEADS):
        hl = slice(h * RET_DK, (h + 1) * RET_DK)
        q_h = rq_ref[:, hl]
        k_h = rk_ref[:, hl]
        v_h = rv_ref[:, hl]
        state = rstate_ref[h]
        scores = _dot_nt(q_h, k_h) * dmask_ref[h]
        y = _dot(scores.astype(BF16), v_h) + _dot(q_h, state.astype(BF16)) * xi_ref[h]
        k_dec = (k_h.astype(F32) * zeta_ref[h]).astype(BF16)
        rstate_ref[h] = state * math.exp(c * log_gamma[h]) + _dot_tn(k_dec, v_h)
        mu = jnp.mean(y, axis=-1, keepdims=True)
        yc = y - mu
        var = jnp.mean(yc * yc, axis=-1, keepdims=True)
        y = yc * lax.rsqrt(var + EPS) * ron_ref[...] * rg_ref[:, hl].astype(F32)
        o_ref[:, GLA_VW + h * RET_DV:GLA_VW + (h + 1) * RET_DV] = y.astype(BF16)


def _mixer(gq, gk, gv, gg, la, rq, rk, rv, rg, gon, ron, batch):
    t = gq.shape[0]
    c = CHUNK
    n_chunks = t // batch // c
    sums_np, masks_np = _gla_structure()
    sums2 = jnp.asarray(np.concatenate([sums_np, sums_np], axis=1), BF16)
    masks = jnp.asarray(masks_np, F32)
    tok = lambda w: pl.BlockSpec((c, w), lambda bi, ci: (bi * n_chunks + ci, 0))
    full = lambda a: pl.BlockSpec(a.shape, lambda bi, ci: (0,) * a.ndim)
    return pl.pallas_call(
        _mixer_kernel,
        grid=(batch, n_chunks),
        in_specs=[tok(GLA_QK), tok(GLA_QK), tok(GLA_VW), tok(GLA_VW), tok(GLA_QK),
                  tok(RET_QK), tok(RET_QK), tok(RET_VW), tok(RET_VW),
                  full(sums2), full(masks), full(gon), full(ron)],
        out_specs=tok(GLA_VW + RET_VW),
        out_shape=jax.ShapeDtypeStruct((t, GLA_VW + RET_VW), BF16),
        scratch_shapes=[
            pltpu.VMEM((GLA_HEADS // 2, GLA_DV, LANES), F32),
            pltpu.VMEM((RET_HEADS, RET_DK, RET_DV), F32),
            pltpu.VMEM((RET_HEADS, c, c), F32),
            pltpu.VMEM((RET_HEADS, c, c), F32),
            pltpu.VMEM((RET_HEADS, c, c), F32),
        ],
        compiler_params=pltpu.CompilerParams(dimension_semantics=("arbitrary", "arbitrary"),
                                             vmem_limit_bytes=_vmem_limit(40 * 1024 * 1024)),
        name="mixer",
    )(gq, gk, gv, gg, la, rq, rk, rv, rg, sums2, masks, gon, ron)


def _post_kernel(o_ref, x_ref, g1_ref, sc_ref, sh_ref, n2_ref, wo_ref, wr_ref, br_ref, tri_ref,
                 x1_ref, h2_ref, idx_ref, wt_ref, rank_ref, cnt_ref, carry_ref):
    @pl.when(pl.program_id(0) == 0)
    def _reset():
        carry_ref[...] = jnp.zeros_like(carry_ref)

    x1 = x_ref[...] + g1_ref[0] * _dot(o_ref[...], wo_ref[...])
    x1_ref[...] = x1
    ms = jnp.mean(x1 * x1, axis=-1, keepdims=True)
    h2 = (x1 * lax.rsqrt(ms + EPS)) * n2_ref[...]
    h2 = h2 * (1.0 + sc_ref[0]) + sh_ref[0]
    half = h2.shape[1] // 2
    h2_ref[...] = _pack_bf16_pair(h2[:, :half], h2[:, half:])

    logits = _dot(h2.astype(BF16), wr_ref[...]) + br_ref[...]
    tm = logits.shape[0]
    lane = lax.broadcasted_iota(I32, (tm, LANES), 1)
    neg_inf = jnp.float32(-jnp.inf)
    logits = jnp.where(lane < N_EXPERTS, logits, neg_inf)
    vals, hots = [], []
    idx_slab = jnp.zeros((tm, LANES), I32)
    for kk in range(TOP_K):
        m = jnp.max(logits, axis=-1, keepdims=True)
        first = jnp.min(jnp.where(logits == m, lane, LANES), axis=-1, keepdims=True)
        hot = lane == first
        logits = jnp.where(hot, neg_inf, logits)
        vals.append(m)
        hots.append(hot)
        idx_slab = jnp.where(lane == kk, first, idx_slab)
    exps = [jnp.exp(v - vals[0]) for v in vals]
    denom = exps[0] + exps[1] + exps[2] + exps[3]
    wt_slab = jnp.zeros((tm, LANES), F32)
    for kk in range(TOP_K):
        wt_slab = jnp.where(lane == kk, exps[kk] / denom, wt_slab)

    sel = jnp.zeros((tm, LANES), F32)
    for hot in hots:
        sel = sel + hot.astype(F32)
    before = _dot(tri_ref[...], sel.astype(BF16)) + carry_ref[...]
    rank_slab = jnp.zeros((tm, LANES), I32)
    for kk in range(TOP_K):
        r = jnp.sum(jnp.where(hots[kk], before, 0.0), axis=-1, keepdims=True)
        rank_slab = jnp.where(lane == kk, r.astype(I32), rank_slab)
    total = before[tm - 1:tm, :] + sel[tm - 1:tm, :]
    carry_ref[...] = total
    cnt_ref[...] = total
    idx_ref[...] = idx_slab
    wt_ref[...] = wt_slab
    rank_ref[...] = rank_slab


def _post(o, x2, g1, sc2, sh2, n2, w_out_b, w_r, b_r, tiles_per_batch):
    t, d = x2.shape
    tm = TM_PROJ
    tri = jnp.asarray(np.tril(np.ones((tm, tm), np.float32), -1), BF16)
    tok = lambda w: pl.BlockSpec((tm, w), lambda i: (i, 0))
    per_batch = pl.BlockSpec((1, 1, d), lambda i: (i // tiles_per_batch, 0, 0))
    full = lambda a: pl.BlockSpec(a.shape, lambda i: (0,) * a.ndim)
    est = 2 * (tm * d * (2 + 4 + 4 + 2) + w_out_b.size * 2 + 3 * tm * LANES * 4) + 12 * tm * d * 4
    return pl.pallas_call(
        _post_kernel,
        grid=(t // tm,),
        in_specs=[tok(d), tok(d), per_batch, per_batch, per_batch, full(n2),
                  full(w_out_b), full(w_r), full(b_r), full(tri)],
        out_specs=[tok(d), tok(d // 2), tok(LANES), tok(LANES), tok(LANES),
                   pl.BlockSpec((1, LANES), lambda i: (0, 0))],
        out_shape=[jax.ShapeDtypeStruct((t, d), F32), jax.ShapeDtypeStruct((t, d // 2), I32),
                   jax.ShapeDtypeStruct((t, LANES), I32), jax.ShapeDtypeStruct((t, LANES), F32),
                   jax.ShapeDtypeStruct((t, LANES), I32), jax.ShapeDtypeStruct((1, LANES), F32)],
        scratch_shapes=[pltpu.VMEM((1, LANES), F32)],
        compiler_params=pltpu.CompilerParams(dimension_semantics=("arbitrary",),
                                             vmem_limit_bytes=_vmem_limit(est)),
        name="post",
    )(o, x2, g1, sc2, sh2, n2, w_out_b, w_r, b_r, tri)


def _dispatch_kernel(pos_ref, h2_ref, xs_in_ref, xs_ref, sem):
    del xs_in_ref
    tm = h2_ref.shape[0]

    def row_copy(r, j):
        dst = pos_ref[0, 0, r * TOP_K + j]
        return pltpu.make_async_copy(h2_ref.at[pl.ds(r, 1)], xs_ref.at[pl.ds(dst, 1)], sem)

    def issue(r, carry):
        for j in range(TOP_K):
            row_copy(r, j).start()
        return carry

    lax.fori_loop(0, tm, issue, 0)

    def drain(r, carry):
        for j in range(TOP_K):
            row_copy(r, j).wait()
        return carry

    lax.fori_loop(0, tm, drain, 0)


def _dispatch(pos_tiles, h2p, n_rows):
    t, w = h2p.shape
    tm = TM_ROW
    xs0 = jnp.zeros((n_rows, w), I32)
    return pl.pallas_call(
        _dispatch_kernel,
        grid=(t // tm,),
        in_specs=[pl.BlockSpec((1, 1, tm * TOP_K), lambda i: (i, 0, 0), memory_space=pltpu.SMEM),
                  pl.BlockSpec((tm, w), lambda i: (i, 0)),
                  pl.BlockSpec(memory_space=pl.ANY)],
        out_specs=pl.BlockSpec(memory_space=pl.ANY),
        out_shape=jax.ShapeDtypeStruct((n_rows, w), I32),
        scratch_shapes=[pltpu.SemaphoreType.DMA(())],
        input_output_aliases={2: 0},
        compiler_params=pltpu.CompilerParams(dimension_semantics=("arbitrary",), has_side_effects=True),
        name="dispatch",
    )(pos_tiles, h2p, xs0)


def _expert_kernel(te_ref, nt_ref, xs_ref, wgu_ref, bgu_ref, wd_ref, bd_ref, ys_ref):
    del te_ref
    i = pl.program_id(0)

    @pl.when(i < nt_ref[0])
    def _compute():
        lo, hi = _unpack_bf16_pair(xs_ref[...])
        x = jnp.concatenate([lo.astype(BF16), hi.astype(BF16)], axis=1)
        gu = _dot(x, wgu_ref[0]) + bgu_ref[0]
        f = gu.shape[1] // 2
        gate = jnp.minimum(gu[:, :f], SWIGLU_LIMIT)
        up = jnp.clip(gu[:, f:], -SWIGLU_LIMIT, SWIGLU_LIMIT)
        act = (up + 1.0) * (gate * jax.nn.sigmoid(SWIGLU_ALPHA * gate))
        y = _dot(act.astype(BF16), wd_ref[0]) + bd_ref[0]
        half = y.shape[1] // 2
        ys_ref[...] = _pack_bf16_pair(y[:, :half], y[:, half:])

    @pl.when(i >= nt_ref[0])
    def _idle():
        ys_ref[...] = jnp.zeros_like(ys_ref)


def _experts(tile_expert, n_tiles, xs, wgu_b, bgu, wd_b, bd):
    n_rows, w = xs.shape
    tm = TM_ROW
    e, d, f2 = wgu_b.shape
    grid_spec = pltpu.PrefetchScalarGridSpec(
        num_scalar_prefetch=2,
        grid=(n_rows // tm,),
        in_specs=[
            pl.BlockSpec((tm, w), lambda i, te, nt: (i, 0)),
            pl.BlockSpec((1, d, f2), lambda i, te, nt: (te[i], 0, 0)),
            pl.BlockSpec((1, 1, f2), lambda i, te, nt: (te[i], 0, 0)),
            pl.BlockSpec((1, f2 // 2, d), lambda i, te, nt: (te[i], 0, 0)),
            pl.BlockSpec((1, 1, d), lambda i, te, nt: (te[i], 0, 0)),
        ],
        out_specs=pl.BlockSpec((tm, w), lambda i, te, nt: (i, 0)),
    )
    est = 2 * (2 * tm * w * 4 + d * f2 * 2 + (f2 // 2) * d * 2) + 6 * tm * f2 * 4
    return pl.pallas_call(
        _expert_kernel,
        grid_spec=grid_spec,
        out_shape=jax.ShapeDtypeStruct((n_rows, w), I32),
        compiler_params=pltpu.CompilerParams(dimension_semantics=("arbitrary",),
                                             vmem_limit_bytes=_vmem_limit(est)),
        name="experts",
    )(tile_expert, n_tiles, xs, wgu_b, bgu, wd_b, bd)


def _combine_kernel(pos_ref, ys_ref, wt_ref, x1_ref, g2_ref, nf_ref, o_ref, buf_ref, sem):
    tm = x1_ref.shape[0]

    def row_copy(r, j):
        src = pos_ref[0, 0, r * TOP_K + j]
        return pltpu.make_async_copy(ys_ref.at[pl.ds(src, 1)], buf_ref.at[j, pl.ds(r, 1)], sem)

    def issue(r, carry):
        for j in range(TOP_K):
            row_copy(r, j).start()
        return carry

    lax.fori_loop(0, tm, issue, 0)

    def drain(r, carry):
        for j in range(TOP_K):
            row_copy(r, j).wait()
        return carry

    lax.fori_loop(0, tm, drain, 0)

    wt = wt_ref[...]
    half = x1_ref.shape[1] // 2
    y_lo = jnp.zeros((tm, half), F32)
    y_hi = jnp.zeros((tm, half), F32)
    for j in range(TOP_K):
        lo, hi = _unpack_bf16_pair(buf_ref[j])
        wj = wt[:, j:j + 1]
        y_lo = y_lo + wj * lo
        y_hi = y_hi + wj * hi
    y = jnp.concatenate([y_lo, y_hi], axis=1)
    x2 = x1_ref[...] + g2_ref[0] * y
    ms = jnp.mean(x2 * x2, axis=-1, keepdims=True)
    o_ref[...] = (x2 * lax.rsqrt(ms + EPS)) * nf_ref[...]


def _combine(pos_tiles, ys, wt, x1, g2, nf, tiles_per_batch):
    t, d = x1.shape
    tm = TM_ROW
    w = ys.shape[1]
    return pl.pallas_call(
        _combine_kernel,
        grid=(t // tm,),
        in_specs=[pl.BlockSpec((1, 1, tm * TOP_K), lambda i: (i, 0, 0), memory_space=pltpu.SMEM),
                  pl.BlockSpec(memory_space=pl.ANY),
                  pl.BlockSpec((tm, LANES), lambda i: (i, 0)),
                  pl.BlockSpec((tm, d), lambda i: (i, 0)),
                  pl.BlockSpec((1, 1, d), lambda i: (i // tiles_per_batch, 0, 0)),
                  pl.BlockSpec((1, d), lambda i: (0, 0))],
        out_specs=pl.BlockSpec((tm, d), lambda i: (i, 0)),
        out_shape=jax.ShapeDtypeStruct((t, d), F32),
        scratch_shapes=[pltpu.VMEM((TOP_K, tm, w), I32), pltpu.SemaphoreType.DMA(())],
        compiler_params=pltpu.CompilerParams(dimension_semantics=("arbitrary",)),
        name="combine",
    )(pos_tiles, ys, wt, x1, g2, nf)


def _layer(x, c, positions, w_ada, b_ada, norm1, w_in, w_gk2, b_gk, gla_onorm, ret_onorm,
           w_out, norm2, w_router, b_router, w_gu, b_gu, w_down, b_down):
    batch, seq, d = x.shape
    t = batch * seq
    assert d == D_MODEL and seq % TM_PROJ == 0 and seq % CHUNK == 0 and seq % TM_ROW == 0
    x2 = x.reshape(t, d)

    c_pad = jnp.pad(c, ((0, 8 - batch), (0, 0)))
    mod = _ada(c_pad, w_ada, b_ada.reshape(1, -1))[:batch]
    sh1, sc1, g1, sh2, sc2, g2 = [m.reshape(batch, 1, d) for m in jnp.split(mod, 6, axis=-1)]

    lr0 = 2 * GLA_QK + 2 * GLA_VW
    w_cat = jnp.concatenate(
        [w_in[:, :lr0], w_in[:, lr0 + GLA_GATE_RANK:],
         jnp.pad(w_in[:, lr0:lr0 + GLA_GATE_RANK], ((0, 0), (0, LANES - GLA_GATE_RANK)))],
        axis=1).astype(BF16)
    wgk = jnp.pad(w_gk2, ((0, LANES - GLA_GATE_RANK), (0, 0))).astype(BF16)
    pos_col = positions.reshape(t, 1).astype(I32)
    gq, gk, gv, gg, la, rq, rk, rv, rg = _inproj(
        x2, sc1, sh1, norm1.reshape(1, d), pos_col, w_cat, wgk, b_gk.reshape(1, -1), seq // TM_PROJ)

    o = _mixer(gq, gk, gv, gg, la, rq, rk, rv, rg,
               gla_onorm.reshape(1, -1), ret_onorm.reshape(1, -1), batch)

    w_r = jnp.pad(w_router, ((0, 0), (0, LANES - N_EXPERTS))).astype(BF16)
    b_r = jnp.pad(b_router, (0, LANES - N_EXPERTS)).reshape(1, LANES)
    x1, h2p, idx_slab, wt_slab, rank_slab, counts = _post(
        o, x2, g1, sc2, sh2, norm2.reshape(1, d), w_out.astype(BF16), w_r, b_r, seq // TM_PROJ)

    tm = TM_ROW
    n_tiles_max = (t * TOP_K) // tm + N_EXPERTS
    cnt = counts[0, :N_EXPERTS].astype(I32)
    tiles_e = (cnt + tm - 1) // tm
    tile_end = jnp.cumsum(tiles_e)
    start_row = (tile_end - tiles_e) * tm
    idx = idx_slab[:, :TOP_K]
    experts = jnp.arange(N_EXPERTS, dtype=I32)
    pos = rank_slab[:, :TOP_K] + jnp.sum(
        jnp.where(idx[:, :, None] == experts[None, None, :], start_row[None, None, :], 0), axis=-1)
    pos_tiles = pos.reshape(t // tm, 1, tm * TOP_K)
    tile_ids = jnp.arange(n_tiles_max, dtype=I32)
    tile_expert = jnp.minimum(
        jnp.sum((tile_ids[:, None] >= tile_end[None, :]).astype(I32), axis=1), N_EXPERTS - 1)
    n_tiles = tile_end[-1:].astype(I32)

    xs = _dispatch(pos_tiles, h2p, n_tiles_max * tm)
    ys = _experts(tile_expert, n_tiles, xs, w_gu.astype(BF16), b_gu[:, None, :],
                  w_down.astype(BF16), b_down[:, None, :])
    return pos_tiles, ys, wt_slab, x1, g2


def kernel(x, c, positions, w_ada, b_ada, norm1, w_in, w_gk2, b_gk, gla_onorm, ret_onorm, w_out,
           norm2, w_router, b_router, w_gu, b_gu, w_down, b_down, norm_f):
    assert w_ada.shape[0] == 1, "single-layer problem"
    batch, seq, d = x.shape
    pos_tiles, ys, wt_slab, x1, g2 = _layer(
        x, c, positions, w_ada[0], b_ada[0], norm1[0], w_in[0], w_gk2[0], b_gk[0], gla_onorm[0],
        ret_onorm[0], w_out[0], norm2[0], w_router[0], b_router[0], w_gu[0], b_gu[0], w_down[0],
        b_down[0])
    out = _combine(pos_tiles, ys, wt_slab, x1, g2, norm_f.reshape(1, d), seq // TM_ROW)
    return out.reshape(batch, seq, d)
```

```python
import functools
import math

import jax
import jax.numpy as jnp
import numpy as np
from jax import lax
from jax.experimental import pallas as pl
from jax.experimental.pallas import tpu as pltpu

F32 = jnp.float32
BF16 = jnp.bfloat16
I32 = jnp.int32

D_MODEL = 1024
GLA_HEADS = 4
GLA_DK = 64
GLA_DV = 128
GLA_GATE_RANK = 16
GLA_GATE_NORM = 16.0
RET_HEADS = 4
RET_DK = 128
RET_DV = 128
ROPE_BASE = 10000.0
N_EXPERTS = 32
TOP_K = 4
SWIGLU_LIMIT = 7.0
SWIGLU_ALPHA = 1.702
EPS = 1e-6
GLA_QK = GLA_HEADS * GLA_DK
GLA_VW = GLA_HEADS * GLA_DV
RET_QK = RET_HEADS * RET_DK
RET_VW = RET_HEADS * RET_DV

LANES = 128
V7X_VMEM_BYTES = 64 * 1024 * 1024

CHUNK = 128
TM_PROJ = 512
TM_ROW = 256


def _vmem_limit(nbytes):
    return int(min(nbytes, V7X_VMEM_BYTES - 8 * 1024 * 1024))


def _dot(a, b):
    return jnp.dot(a, b, preferred_element_type=F32)


def _dot_nt(a, b):
    return lax.dot_general(a, b, (((1,), (1,)), ((), ())), preferred_element_type=F32)


def _dot_tn(a, b):
    return lax.dot_general(a, b, (((0,), (0,)), ((), ())), preferred_element_type=F32)


def _silu(x):
    return x * jax.nn.sigmoid(x)


def _pack_bf16_pair(lo, hi):
    lo_b = lax.bitcast_convert_type(lo.astype(BF16).astype(F32), jnp.uint32)
    hi_b = lax.bitcast_convert_type(hi.astype(BF16).astype(F32), jnp.uint32)
    word = (hi_b & jnp.uint32(0xFFFF0000)) | (lo_b >> 16)
    return lax.bitcast_convert_type(word, I32)


def _unpack_bf16_pair(word_i32):
    word = lax.bitcast_convert_type(word_i32, jnp.uint32)
    lo = lax.bitcast_convert_type(word << 16, F32)
    hi = lax.bitcast_convert_type(word & jnp.uint32(0xFFFF0000), F32)
    return lo, hi


def _ada_kernel(c_ref, w_ref, b_ref, o_ref):
    s = _silu(c_ref[...]).astype(BF16)
    o_ref[...] = _dot(s, w_ref[...].astype(BF16)) + b_ref[...]


def _ada(c_pad, w_ada, b_ada):
    rows, d = c_pad.shape
    n = w_ada.shape[1]
    tn = 1024
    return pl.pallas_call(
        _ada_kernel,
        grid=(n // tn,),
        in_specs=[
            pl.BlockSpec((rows, d), lambda j: (0, 0)),
            pl.BlockSpec((d, tn), lambda j: (0, j)),
            pl.BlockSpec((1, tn), lambda j: (0, j)),
        ],
        out_specs=pl.BlockSpec((rows, tn), lambda j: (0, j)),
        out_shape=jax.ShapeDtypeStruct((rows, n), F32),
        compiler_params=pltpu.CompilerParams(dimension_semantics=("arbitrary",)),
        name="ada",
    )(c_pad, w_ada, b_ada)


_O_GQ, _O_GK, _O_GV, _O_GG = 0, 256, 512, 1024
_O_RQ, _O_RK, _O_RV, _O_RG, _O_LR = 1536, 2048, 2560, 3072, 3584
_W_IN_COLS = 3584 + LANES


def _inproj_kernel(x_ref, sc_ref, sh_ref, n1_ref, pos_ref, w_ref, wgk_ref, bgk_ref,
                   gq_ref, gk_ref, gv_ref, gg_ref, la_ref, rq_ref, rk_ref, rv_ref, rg_ref):
    x = x_ref[...]
    ms = jnp.mean(x * x, axis=-1, keepdims=True)
    h = (x * lax.rsqrt(ms + EPS)) * n1_ref[...]
    h = h * (1.0 + sc_ref[0]) + sh_ref[0]
    hb = h.astype(BF16)

    def proj(off, width):
        return _dot(hb, w_ref[:, off:off + width])

    gq_ref[...] = (proj(_O_GQ, GLA_QK) * (GLA_DK ** -0.5)).astype(BF16)
    gk_ref[...] = proj(_O_GK, GLA_QK).astype(BF16)
    gv_ref[...] = proj(_O_GV, GLA_VW).astype(BF16)
    gg_ref[...] = _silu(proj(_O_GG, GLA_VW)).astype(BF16)
    rv_ref[...] = proj(_O_RV, RET_VW).astype(BF16)
    rg_ref[...] = _silu(proj(_O_RG, RET_VW)).astype(BF16)

    glr = proj(_O_LR, LANES).astype(BF16)
    z = _dot(glr, wgk_ref[...]) + bgk_ref[...]
    la_ref[...] = (jnp.minimum(z, 0.0) - jnp.log1p(jnp.exp(-jnp.abs(z)))) * (1.0 / GLA_GATE_NORM)

    half = RET_DK // 2
    lane = lax.broadcasted_iota(I32, (1, RET_DK), 1)
    freq = (lane % half).astype(F32) * (1.0 / half)
    inv = jnp.exp(freq * (-math.log(ROPE_BASE)))
    ang = pos_ref[...].astype(F32) * inv
    cos = jnp.cos(ang)
    sin = jnp.sin(ang)
    sin_signed = jnp.where(lane < half, -sin, sin)
    rq = proj(_O_RQ, RET_QK)
    rk = proj(_O_RK, RET_QK)
    for hd in range(RET_HEADS):
        sl = slice(hd * RET_DK, (hd + 1) * RET_DK)
        q_h = rq[:, sl]
        k_h = rk[:, sl]
        rq_ref[:, sl] = (q_h * cos + pltpu.roll(q_h, half, 1) * sin_signed).astype(BF16)
        rk_ref[:, sl] = ((k_h * cos + pltpu.roll(k_h, half, 1) * sin_signed)
                         * (RET_DK ** -0.5)).astype(BF16)


def _inproj(x2, sc1, sh1, n1, pos_col, w_cat, wgk, bgk, tiles_per_batch):
    t, d = x2.shape
    tm = TM_PROJ
    tok = lambda w: pl.BlockSpec((tm, w), lambda i: (i, 0))
    per_batch = pl.BlockSpec((1, 1, d), lambda i: (i // tiles_per_batch, 0, 0))
    full = lambda a: pl.BlockSpec(a.shape, lambda i: (0,) * a.ndim)
    out_widths = [GLA_QK, GLA_QK, GLA_VW, GLA_VW, GLA_QK, RET_QK, RET_QK, RET_VW, RET_VW]
    out_dtypes = [BF16, BF16, BF16, BF16, F32, BF16, BF16, BF16, BF16]
    est = 2 * (tm * d * 4 + w_cat.size * 2 + sum(tm * w * 4 for w in out_widths)) + 16 * tm * 512 * 4
    return pl.pallas_call(
        _inproj_kernel,
        grid=(t // tm,),
        in_specs=[tok(d), per_batch, per_batch, full(n1), tok(1), full(w_cat), full(wgk), full(bgk)],
        out_specs=[tok(w) for w in out_widths],
        out_shape=[jax.ShapeDtypeStruct((t, w), dt) for w, dt in zip(out_widths, out_dtypes)],
        compiler_params=pltpu.CompilerParams(dimension_semantics=("arbitrary",),
                                             vmem_limit_bytes=_vmem_limit(est)),
        name="inproj",
    )(x2, sc1, sh1, n1, pos_col, w_cat, wgk, bgk)


_GLA_LEVELS = int(math.log2(CHUNK))


def _gla_structure():
    c = CHUNK
    t = np.arange(c)[:, None]
    u = np.arange(c)[None, :]
    groups = [u <= t, u > t]
    q_side, k_side, masks = [], [], []
    half = c // 2
    while half >= 1:
        blk = 2 * half
        mid = t - (t % blk) + half
        upper = (t % blk) >= half
        q_side.append(upper & (u >= mid) & (u <= t))
        k_side.append((~upper) & (u > t) & (u <= mid - 1))
        masks.append((t // blk == u // blk) & upper & ((u % blk) < half))
        half //= 2
    masks.append(np.eye(c, dtype=bool))
    sums = np.concatenate(groups + q_side + k_side, axis=0).astype(np.float32)
    return sums, np.stack(masks).astype(np.float32)


_LOG_GAMMA = [math.log1p(-(2.0 ** (-5 - h))) for h in range(RET_HEADS)]


def _mixer_kernel(gq_ref, gk_ref, gv_ref, gg_ref, la_ref, rq_ref, rk_ref, rv_ref, rg_ref,
                  sums_ref, mask_ref, gon_ref, ron_ref, o_ref,
                  gstate_ref, rstate_ref, dmask_ref, xi_ref, zeta_ref):
    c = CHUNK
    chunk_id = pl.program_id(0)
    log_gamma = _LOG_GAMMA

    @pl.when(chunk_id == 0)
    def _init_decay_tables():
        t = lax.broadcasted_iota(I32, (c, c), 0)
        s = lax.broadcasted_iota(I32, (c, c), 1)
        rel = (t - s).astype(F32)
        tf = t.astype(F32)
        for h in range(RET_HEADS):
            dmask_ref[h] = jnp.where(rel >= 0.0, jnp.exp(jnp.maximum(rel, 0.0) * log_gamma[h]), 0.0)
            xi_ref[h] = jnp.exp((tf + 1.0) * log_gamma[h])
            zeta_ref[h] = jnp.exp((c - 1.0 - tf) * log_gamma[h])

    @pl.when(chunk_id == 0)
    def _reset_state():
        gstate_ref[...] = jnp.zeros_like(gstate_ref)
        rstate_ref[...] = jnp.zeros_like(rstate_ref)

    for bi in range(o_ref.shape[0]):
        _mixer_chunk(gq_ref.at[bi], gk_ref.at[bi], gv_ref.at[bi], gg_ref.at[bi], la_ref.at[bi],
                     rq_ref.at[bi], rk_ref.at[bi], rv_ref.at[bi], rg_ref.at[bi],
                     sums_ref, mask_ref, gon_ref, ron_ref, o_ref.at[bi],
                     gstate_ref.at[bi], rstate_ref.at[bi], dmask_ref, xi_ref, zeta_ref)


def _mixer_chunk(gq_ref, gk_ref, gv_ref, gg_ref, la_ref, rq_ref, rk_ref, rv_ref, rg_ref,
                 sums_ref, mask_ref, gon_ref, ron_ref, o_ref,
                 gstate_ref, rstate_ref, dmask_ref, xi_ref, zeta_ref):
    c = CHUNK
    log_gamma = _LOG_GAMMA

    la = la_ref[...]
    la_hi = la.astype(BF16)
    la_lo = (la - la_hi.astype(F32)).astype(BF16)
    expo = _dot(sums_ref[...], jnp.concatenate([la_hi, la_lo], axis=0))
    qb = gq_ref[...]
    kb = gk_ref[...]
    q = qb.astype(F32)
    k = kb.astype(F32)
    b = expo[0:c]
    q_inter = (q * jnp.exp(b)).astype(BF16)
    k_state = (k * jnp.exp(expo[c:2 * c])).astype(BF16)
    decay_last = jnp.exp(b[c - 1:c, :])
    q_lv, k_lv = [], []
    for l in range(_GLA_LEVELS):
        q_lv.append((q * jnp.exp(expo[(2 + l) * c:(3 + l) * c])).astype(BF16))
        k_lv.append((k * jnp.exp(expo[(2 + _GLA_LEVELS + l) * c:(3 + _GLA_LEVELS + l) * c])).astype(BF16))
    q_lv.append(qb)
    k_lv.append(kb)

    lane = lax.broadcasted_iota(I32, (c, LANES), 1)
    head_lanes = [lane < GLA_DK, lane >= GLA_DK]
    zero_b = jnp.zeros((c, LANES), BF16)
    for g in range(GLA_HEADS // 2):
        gl = slice(g * LANES, (g + 1) * LANES)
        state_t = gstate_ref[g]
        state_b = state_t.astype(BF16)
        new_state = state_t * decay_last[:, gl]
        for hh in range(2):
            h = 2 * g + hh
            vl = slice(h * GLA_DV, (h + 1) * GLA_DV)
            keep = head_lanes[hh]
            attn = jnp.zeros((c, c), F32)
            for l in range(_GLA_LEVELS + 1):
                q_h = jnp.where(keep, q_lv[l][:, gl], zero_b)
                attn = attn + mask_ref[l] * _dot_nt(q_h, k_lv[l][:, gl])
            v_h = gv_ref[:, vl]
            o_h = _dot(attn.astype(BF16), v_h)
            o_h = o_h + _dot_nt(jnp.where(keep, q_inter[:, gl], zero_b), state_b)
            new_state = new_state + _dot_tn(v_h, jnp.where(keep, k_state[:, gl], zero_b))
            ms = jnp.mean(o_h * o_h, axis=-1, keepdims=True)
            o_h = o_h * lax.rsqrt(ms + EPS) * gon_ref[...] * gg_ref[:, vl].astype(F32)
            o_ref[:, vl] = o_h.astype(BF16)
        gstate_ref[g] = new_state

    for h in range(RET_HEADS):
        hl = slice(h * RET_DK, (h + 1) * RET_DK)
        q_h = rq_ref[:, hl]
        k_h = rk_ref[:, hl]
        v_h = rv_ref[:, hl]
        state = rstate_ref[h]
        scores = _dot_nt(q_h, k_h) * dmask_ref[h]
        y = _dot(scores.astype(BF16), v_h) + _dot(q_h, state.astype(BF16)) * xi_ref[h]
        k_dec = (k_h.astype(F32) * zeta_ref[h]).astype(BF16)
        rstate_ref[h] = state * math.exp(c * log_gamma[h]) + _dot_tn(k_dec, v_h)
        mu = jnp.mean(y, axis=-1, keepdims=True)
        yc = y - mu
        var = jnp.mean(yc * yc, axis=-1, keepdims=True)
        y = yc * lax.rsqrt(var + EPS) * ron_ref[...] * rg_ref[:, hl].astype(F32)
        o_ref[:, GLA_VW + h * RET_DV:GLA_VW + (h + 1) * RET_DV] = y.astype(BF16)


def _mixer(gq, gk, gv, gg, la, rq, rk, rv, rg, gon, ron, batch):
    t = gq.shape[0]
    c = CHUNK
    seq = t // batch
    sums_np, masks_np = _gla_structure()
    sums2 = jnp.asarray(np.concatenate([sums_np, sums_np], axis=1), BF16)
    masks = jnp.asarray(masks_np, F32)
    tok = lambda w: pl.BlockSpec((batch, c, w), lambda ci: (0, ci, 0))
    full = lambda a: pl.BlockSpec(a.shape, lambda ci: (0,) * a.ndim)
    per_seq = lambda a: a.reshape(batch, seq, a.shape[1])
    out = pl.pallas_call(
        _mixer_kernel,
        grid=(seq // c,),
        in_specs=[tok(GLA_QK), tok(GLA_QK), tok(GLA_VW), tok(GLA_VW), tok(GLA_QK),
                  tok(RET_QK), tok(RET_QK), tok(RET_VW), tok(RET_VW),
                  full(sums2), full(masks), full(gon), full(ron)],
        out_specs=tok(GLA_VW + RET_VW),
        out_shape=jax.ShapeDtypeStruct((batch, seq, GLA_VW + RET_VW), BF16),
        scratch_shapes=[
            pltpu.VMEM((batch, GLA_HEADS // 2, GLA_DV, LANES), F32),
            pltpu.VMEM((batch, RET_HEADS, RET_DK, RET_DV), F32),
            pltpu.VMEM((RET_HEADS, c, c), F32),
            pltpu.VMEM((RET_HEADS, c, c), F32),
            pltpu.VMEM((RET_HEADS, c, c), F32),
        ],
        compiler_params=pltpu.CompilerParams(dimension_semantics=("arbitrary",),
                                             vmem_limit_bytes=_vmem_limit(40 * 1024 * 1024)),
        name="mixer",
    )(*[per_seq(a) for a in (gq, gk, gv, gg, la, rq, rk, rv, rg)], sums2, masks, gon, ron)
    return out.reshape(t, GLA_VW + RET_VW)


def _post_kernel(o_ref, x_ref, g1_ref, sc_ref, sh_ref, n2_ref, wo_ref, wr_ref, br_ref, tri_ref,
                 x1_ref, h2_ref, idx_ref, wt_ref, rank_ref, cnt_ref, carry_ref):
    @pl.when(pl.program_id(0) == 0)
    def _reset():
        carry_ref[...] = jnp.zeros_like(carry_ref)

    x1 = x_ref[...] + g1_ref[0] * _dot(o_ref[...], wo_ref[...])
    x1_ref[...] = x1
    ms = jnp.mean(x1 * x1, axis=-1, keepdims=True)
    h2 = (x1 * lax.rsqrt(ms + EPS)) * n2_ref[...]
    h2 = h2 * (1.0 + sc_ref[0]) + sh_ref[0]
    half = h2.shape[1] // 2
    h2_ref[...] = _pack_bf16_pair(h2[:, :half], h2[:, half:])

    logits = _dot(h2.astype(BF16), wr_ref[...]) + br_ref[...]
    tm = logits.shape[0]
    lane = lax.broadcasted_iota(I32, (tm, LANES), 1)
    neg_inf = jnp.float32(-jnp.inf)
    logits = jnp.where(lane < N_EXPERTS, logits, neg_inf)
    vals, hots = [], []
    idx_slab = jnp.zeros((tm, LANES), I32)
    for kk in range(TOP_K):
        m = jnp.max(logits, axis=-1, keepdims=True)
        first = jnp.min(jnp.where(logits == m, lane, LANES), axis=-1, keepdims=True)
        hot = lane == first
        logits = jnp.where(hot, neg_inf, logits)
        vals.append(m)
        hots.append(hot)
        idx_slab = jnp.where(lane == kk, first, idx_slab)
    exps = [jnp.exp(v - vals[0]) for v in vals]
    denom = exps[0] + exps[1] + exps[2] + exps[3]
    wt_slab = jnp.zeros((tm, LANES), F32)
    for kk in range(TOP_K):
        wt_slab = jnp.where(lane == kk, exps[kk] / denom, wt_slab)

    sel = jnp.zeros((tm, LANES), F32)
    for hot in hots:
        sel = sel + hot.astype(F32)
    before = _dot(tri_ref[...], sel.astype(BF16)) + carry_ref[...]
    rank_slab = jnp.zeros((tm, LANES), I32)
    for kk in range(TOP_K):
        r = jnp.sum(jnp.where(hots[kk], before, 0.0), axis=-1, keepdims=True)
        rank_slab = jnp.where(lane == kk, r.astype(I32), rank_slab)
    total = before[tm - 1:tm, :] + sel[tm - 1:tm, :]
    carry_ref[...] = total
    cnt_ref[...] = total
    idx_ref[...] = idx_slab
    wt_ref[...] = wt_slab
    rank_ref[...] = rank_slab


def _post(o, x2, g1, sc2, sh2, n2, w_out_b, w_r, b_r, tiles_per_batch):
    t, d = x2.shape
    tm = TM_PROJ
    tri = jnp.asarray(np.tril(np.ones((tm, tm), np.float32), -1), BF16)
    tok = lambda w: pl.BlockSpec((tm, w), lambda i: (i, 0))
    per_batch = pl.BlockSpec((1, 1, d), lambda i: (i // tiles_per_batch, 0, 0))
    full = lambda a: pl.BlockSpec(a.shape, lambda i: (0,) * a.ndim)
    est = 2 * (tm * d * (2 + 4 + 4 + 2) + w_out_b.size * 2 + 3 * tm * LANES * 4) + 12 * tm * d * 4
    return pl.pallas_call(
        _post_kernel,
        grid=(t // tm,),
        in_specs=[tok(d), tok(d), per_batch, per_batch, per_batch, full(n2),
                  full(w_out_b), full(w_r), full(b_r), full(tri)],
        out_specs=[tok(d), tok(d // 2), tok(LANES), tok(LANES), tok(LANES),
                   pl.BlockSpec((1, LANES), lambda i: (0, 0))],
        out_shape=[jax.ShapeDtypeStruct((t, d), F32), jax.ShapeDtypeStruct((t, d // 2), I32),
                   jax.ShapeDtypeStruct((t, LANES), I32), jax.ShapeDtypeStruct((t, LANES), F32),
                   jax.ShapeDtypeStruct((t, LANES), I32), jax.ShapeDtypeStruct((1, LANES), F32)],
        scratch_shapes=[pltpu.VMEM((1, LANES), F32)],
        compiler_params=pltpu.CompilerParams(dimension_semantics=("arbitrary",),
                                             vmem_limit_bytes=_vmem_limit(est)),
        name="post",
    )(o, x2, g1, sc2, sh2, n2, w_out_b, w_r, b_r, tri)


_ROW_UNROLL = 8


def _dispatch_kernel(pos_ref, pad_ref, h2_ref, xs_ref, zero_ref, sem, zero_sem):
    tm = h2_ref.shape[0]
    step = pl.program_id(0)

    def zero_copy(r):
        return pltpu.make_async_copy(zero_ref.at[pl.ds(0, 1)], xs_ref.at[pl.ds(r, 1)], zero_sem)

    def zero_tile_copy(i):
        row0 = pl.multiple_of(i * tm, tm)
        return pltpu.make_async_copy(zero_ref, xs_ref.at[pl.ds(row0, tm)], zero_sem)

    @pl.when(step == 0)
    def _zero_padding_rows():
        zero_ref[...] = jnp.zeros_like(zero_ref)
        first_idle = pad_ref[2, 0]
        n_tiles_max = xs_ref.shape[0] // tm

        def start_tile(i, c):
            zero_tile_copy(i).start()
            return c

        def wait_tile(i, c):
            zero_tile_copy(i).wait()
            return c

        lax.fori_loop(first_idle, n_tiles_max, start_tile, 0)
        lax.fori_loop(first_idle, n_tiles_max, wait_tile, 0)

        def per_expert(e, carry):
            lo = pad_ref[0, e]
            hi = pad_ref[1, e]

            def start(r, c):
                zero_copy(r).start()
                return c

            def wait(r, c):
                zero_copy(r).wait()
                return c

            lax.fori_loop(lo, hi, start, 0)
            lax.fori_loop(lo, hi, wait, 0)
            return carry

        lax.fori_loop(0, N_EXPERTS, per_expert, 0)

    def row_copy(r, j):
        dst = pos_ref[0, 0, r * TOP_K + j]
        return pltpu.make_async_copy(h2_ref.at[pl.ds(r, 1)], xs_ref.at[pl.ds(dst, 1)], sem)

    def issue(r, carry):
        for j in range(TOP_K):
            row_copy(r, j).start(priority=j % 2)
        return carry

    lax.fori_loop(0, tm, issue, 0, unroll=_ROW_UNROLL)

    def drain(r, carry):
        for j in range(TOP_K):
            row_copy(r, j).wait()
        return carry

    lax.fori_loop(0, tm, drain, 0, unroll=_ROW_UNROLL)


def _dispatch(pos_tiles, pad_rows, h2p, n_rows):
    t, w = h2p.shape
    tm = TM_ROW
    return pl.pallas_call(
        _dispatch_kernel,
        grid=(t // tm,),
        in_specs=[pl.BlockSpec((1, 1, tm * TOP_K), lambda i: (i, 0, 0), memory_space=pltpu.SMEM),
                  pl.BlockSpec(memory_space=pltpu.SMEM),
                  pl.BlockSpec((tm, w), lambda i: (i, 0))],
        out_specs=pl.BlockSpec(memory_space=pl.ANY),
        out_shape=jax.ShapeDtypeStruct((n_rows, w), I32),
        scratch_shapes=[pltpu.VMEM((tm, w), I32), pltpu.SemaphoreType.DMA(()), pltpu.SemaphoreType.DMA(())],
        compiler_params=pltpu.CompilerParams(dimension_semantics=("arbitrary",), has_side_effects=True),
        name="dispatch",
    )(pos_tiles, pad_rows, h2p)


_CAST_ROWS = 128


def _expert_kernel(te_ref, nt_ref, xs_ref, wgu_ref, bgu_ref, wd_ref, bd_ref, ys_ref, wgu_b_ref, wd_b_ref):
    i = pl.program_id(0)
    active = i < nt_ref[0]
    new_expert = (i == 0) | (te_ref[i] != te_ref[jnp.maximum(i - 1, 0)])

    @pl.when(active & new_expert)
    def _convert_weights():
        def cast_gu(r, c):
            rows = pl.ds(pl.multiple_of(r * _CAST_ROWS, _CAST_ROWS), _CAST_ROWS)
            wgu_b_ref[rows, :] = wgu_ref[0, rows, :].astype(BF16)
            return c

        def cast_down(r, c):
            rows = pl.ds(pl.multiple_of(r * _CAST_ROWS, _CAST_ROWS), _CAST_ROWS)
            wd_b_ref[rows, :] = wd_ref[0, rows, :].astype(BF16)
            return c

        lax.fori_loop(0, wgu_b_ref.shape[0] // _CAST_ROWS, cast_gu, 0)
        lax.fori_loop(0, wd_b_ref.shape[0] // _CAST_ROWS, cast_down, 0)

    @pl.when(active)
    def _compute():
        lo, hi = _unpack_bf16_pair(xs_ref[...])
        x = jnp.concatenate([lo.astype(BF16), hi.astype(BF16)], axis=1)
        gu = _dot(x, wgu_b_ref[...]) + bgu_ref[0]
        f = gu.shape[1] // 2
        gate = jnp.minimum(gu[:, :f], SWIGLU_LIMIT)
        up = jnp.clip(gu[:, f:], -SWIGLU_LIMIT, SWIGLU_LIMIT)
        act = (up + 1.0) * (gate * jax.nn.sigmoid(SWIGLU_ALPHA * gate))
        y = _dot(act.astype(BF16), wd_b_ref[...]) + bd_ref[0]
        half = y.shape[1] // 2
        ys_ref[...] = _pack_bf16_pair(y[:, :half], y[:, half:])

    @pl.when(jnp.logical_not(active))
    def _idle():
        ys_ref[...] = jnp.zeros_like(ys_ref)


def _experts(tile_expert, n_tiles, xs, wgu, bgu, wd, bd):
    n_rows, w = xs.shape
    tm = TM_ROW
    e, d, f2 = wgu.shape
    f = f2 // 2
    grid_spec = pltpu.PrefetchScalarGridSpec(
        num_scalar_prefetch=2,
        grid=(n_rows // tm,),
        in_specs=[
            pl.BlockSpec((tm, w), lambda i, te, nt: (jnp.minimum(i, nt[0] - 1), 0)),
            pl.BlockSpec((1, d, f2), lambda i, te, nt: (te[i], 0, 0)),
            pl.BlockSpec((1, 1, f2), lambda i, te, nt: (te[i], 0, 0)),
            pl.BlockSpec((1, f, d), lambda i, te, nt: (te[i], 0, 0)),
            pl.BlockSpec((1, 1, d), lambda i, te, nt: (te[i], 0, 0)),
        ],
        out_specs=pl.BlockSpec((tm, w), lambda i, te, nt: (i, 0)),
        scratch_shapes=[pltpu.VMEM((d, f2), BF16), pltpu.VMEM((f, d), BF16)],
    )
    est = 2 * (2 * tm * w * 4 + d * f2 * 4 + f * d * 4) + (d * f2 + f * d) * 2 + 6 * tm * f2 * 4
    return pl.pallas_call(
        _expert_kernel,
        grid_spec=grid_spec,
        out_shape=jax.ShapeDtypeStruct((n_rows, w), I32),
        compiler_params=pltpu.CompilerParams(dimension_semantics=("arbitrary",),
                                             vmem_limit_bytes=_vmem_limit(est)),
        name="experts",
    )(tile_expert, n_tiles, xs, wgu, bgu, wd, bd)


def _combine_kernel(pos_ref, pos_next_ref, ys_ref, wt_ref, x1_ref, g2_ref, nf_ref, o_ref, buf_ref, sem):
    tm = x1_ref.shape[0]
    i = pl.program_id(0)
    slot = i % 2

    def row_copy(p_ref, s, r, j):
        src = p_ref[0, 0, r * TOP_K + j]
        return pltpu.make_async_copy(ys_ref.at[pl.ds(src, 1)], buf_ref.at[s, j, pl.ds(r, 1)], sem.at[s])

    def issue_rows(p_ref, s):
        def issue(r, carry):
            for j in range(TOP_K):
                row_copy(p_ref, s, r, j).start(priority=j % 2)
            return carry

        lax.fori_loop(0, tm, issue, 0, unroll=_ROW_UNROLL)

    @pl.when(i == 0)
    def _first_fetch():
        issue_rows(pos_ref, 0)

    @pl.when(i + 1 < pl.num_programs(0))
    def _next_fetch():
        issue_rows(pos_next_ref, 1 - slot)

    def drain(r, carry):
        for j in range(TOP_K):
            row_copy(pos_ref, slot, r, j).wait()
        return carry

    lax.fori_loop(0, tm, drain, 0, unroll=_ROW_UNROLL)

    wt = wt_ref[...]
    half = x1_ref.shape[1] // 2
    y_lo = jnp.zeros((tm, half), F32)
    y_hi = jnp.zeros((tm, half), F32)
    for j in range(TOP_K):
        lo, hi = _unpack_bf16_pair(buf_ref[slot, j])
        wj = wt[:, j:j + 1]
        y_lo = y_lo + wj * lo
        y_hi = y_hi + wj * hi
    y = jnp.concatenate([y_lo, y_hi], axis=1)
    x2 = x1_ref[...] + g2_ref[0] * y
    ms = jnp.mean(x2 * x2, axis=-1, keepdims=True)
    o_ref[...] = (x2 * lax.rsqrt(ms + EPS)) * nf_ref[...]


def _combine(pos_tiles, ys, wt, x1, g2, nf, tiles_per_batch):
    t, d = x1.shape
    tm = TM_ROW
    w = ys.shape[1]
    n_steps = t // tm
    pos_spec = lambda f: pl.BlockSpec((1, 1, tm * TOP_K), f, memory_space=pltpu.SMEM)
    return pl.pallas_call(
        _combine_kernel,
        grid=(n_steps,),
        in_specs=[pos_spec(lambda i: (i, 0, 0)),
                  pos_spec(lambda i: (jnp.minimum(i + 1, n_steps - 1), 0, 0)),
                  pl.BlockSpec(memory_space=pl.ANY),
                  pl.BlockSpec((tm, LANES), lambda i: (i, 0)),
                  pl.BlockSpec((tm, d), lambda i: (i, 0)),
                  pl.BlockSpec((1, 1, d), lambda i: (i // tiles_per_batch, 0, 0)),
                  pl.BlockSpec((1, d), lambda i: (0, 0))],
        out_specs=pl.BlockSpec((tm, d), lambda i: (i, 0)),
        out_shape=jax.ShapeDtypeStruct((t, d), F32),
        scratch_shapes=[pltpu.VMEM((2, TOP_K, tm, w), I32), pltpu.SemaphoreType.DMA((2,))],
        compiler_params=pltpu.CompilerParams(dimension_semantics=("arbitrary",)),
        name="combine",
    )(pos_tiles, pos_tiles, ys, wt, x1, g2, nf)


def _layer(x, c, positions, w_ada, b_ada, norm1, w_in, w_gk2, b_gk, gla_onorm, ret_onorm,
           w_out, norm2, w_router, b_router, w_gu, b_gu, w_down, b_down):
    batch, seq, d = x.shape
    t = batch * seq
    assert d == D_MODEL and seq % TM_PROJ == 0 and seq % CHUNK == 0 and seq % TM_ROW == 0
    x2 = x.reshape(t, d)

    c_pad = jnp.pad(c, ((0, 8 - batch), (0, 0)))
    mod = _ada(c_pad, w_ada, b_ada.reshape(1, -1))[:batch]
    sh1, sc1, g1, sh2, sc2, g2 = [m.reshape(batch, 1, d) for m in jnp.split(mod, 6, axis=-1)]

    lr0 = 2 * GLA_QK + 2 * GLA_VW
    w_cat = jnp.concatenate(
        [w_in[:, :lr0], w_in[:, lr0 + GLA_GATE_RANK:],
         jnp.pad(w_in[:, lr0:lr0 + GLA_GATE_RANK], ((0, 0), (0, LANES - GLA_GATE_RANK)))],
        axis=1).astype(BF16)
    wgk = jnp.pad(w_gk2, ((0, LANES - GLA_GATE_RANK), (0, 0))).astype(BF16)
    pos_col = positions.reshape(t, 1).astype(I32)
    gq, gk, gv, gg, la, rq, rk, rv, rg = _inproj(
        x2, sc1, sh1, norm1.reshape(1, d), pos_col, w_cat, wgk, b_gk.reshape(1, -1), seq // TM_PROJ)

    o = _mixer(gq, gk, gv, gg, la, rq, rk, rv, rg,
               gla_onorm.reshape(1, -1), ret_onorm.reshape(1, -1), batch)

    w_r = jnp.pad(w_router, ((0, 0), (0, LANES - N_EXPERTS))).astype(BF16)
    b_r = jnp.pad(b_router, (0, LANES - N_EXPERTS)).reshape(1, LANES)
    x1, h2p, idx_slab, wt_slab, rank_slab, counts = _post(
        o, x2, g1, sc2, sh2, norm2.reshape(1, d), w_out.astype(BF16), w_r, b_r, seq // TM_PROJ)

    tm = TM_ROW
    n_tiles_max = (t * TOP_K) // tm + N_EXPERTS
    cnt = counts[0, :N_EXPERTS].astype(I32)
    tiles_e = (cnt + tm - 1) // tm
    tile_end = jnp.cumsum(tiles_e)
    start_row = (tile_end - tiles_e) * tm
    idx = idx_slab[:, :TOP_K]
    experts = jnp.arange(N_EXPERTS, dtype=I32)
    pos = rank_slab[:, :TOP_K] + jnp.sum(
        jnp.where(idx[:, :, None] == experts[None, None, :], start_row[None, None, :], 0), axis=-1)
    pos_tiles = pos.reshape(t // tm, 1, tm * TOP_K)
    tile_ids = jnp.arange(n_tiles_max, dtype=I32)
    tile_expert = jnp.minimum(
        jnp.sum((tile_ids[:, None] >= tile_end[None, :]).astype(I32), axis=1), N_EXPERTS - 1)
    n_tiles = tile_end[-1:].astype(I32)

    pad_rows = jnp.stack([start_row + cnt, start_row + tiles_e * tm,
                          jnp.broadcast_to(n_tiles, (N_EXPERTS,))]).astype(I32)

    xs = _dispatch(pos_tiles, pad_rows, h2p, n_tiles_max * tm)
    ys = _experts(tile_expert, n_tiles, xs, w_gu, b_gu[:, None, :], w_down, b_down[:, None, :])
    return pos_tiles, ys, wt_slab, x1, g2


def kernel(x, c, positions, w_ada, b_ada, norm1, w_in, w_gk2, b_gk, gla_onorm, ret_onorm, w_out,
           norm2, w_router, b_router, w_gu, b_gu, w_down, b_down, norm_f):
    assert w_ada.shape[0] == 1, "single-layer problem"
    batch, seq, d = x.shape
    pos_tiles, ys, wt_slab, x1, g2 = _layer(
        x, c, positions, w_ada[0], b_ada[0], norm1[0], w_in[0], w_gk2[0], b_gk[0], gla_onorm[0],
        ret_onorm[0], w_out[0], norm2[0], w_router[0], b_router[0], w_gu[0], b_gu[0], w_down[0],
        b_down[0])
    out = _combine(pos_tiles, ys, wt_slab, x1, g2, norm_f.reshape(1, d), seq // TM_ROW)
    return out.reshape(batch, seq, d)
```
